```python
import jax, jax.numpy as jnp
from jax import lax
import numpy as np

D_MODEL = 1024
BATCH = 16
SEQ = 2048
DEPTH = 1
DEC_BATCH = 8
DEC_SEQ = 32
PAST_LEN = 1024

CHUNK = 64
D_MIX = D_MODEL
D_POOL = D_MIX // 2
POOL_WINDOWS = (2, 4, 8, 16)
N_POOL_GROUPS = len(POOL_WINDOWS)
POOL_GROUP = D_POOL // N_POOL_GROUPS
POOL_BUF = max(POOL_WINDOWS) - 1
D_LRU = D_MIX - D_POOL
N_LRU_HEADS = 8
LRU_HEAD = D_LRU // N_LRU_HEADS
CONV_W = 4
LRU_C = 8.0
D_IN = D_POOL + 2 * D_LRU
PEER_HEADS = 8
N_KEYS = 128
N_EXPERTS = N_KEYS * N_KEYS
D_KEY = 256
D_HALF = D_KEY // 2
PEER_TOPK = 16
PEER_BLOCK = 256
EPS = 1e-6

kernel_name = 'hymba_pool_rglru_peer_stream_step'


def rmsnorm(x, g):
    xf = x.astype(jnp.float32)
    y = xf * lax.rsqrt(jnp.mean(xf * xf, axis=-1, keepdims=True) + EPS)
    return (y * g.astype(jnp.float32)).astype(x.dtype)


def pool_mixer(u, buf, start_pos, w_pool, pool_scale):
    T = u.shape[1]
    full = jnp.concatenate([buf.astype(u.dtype), u], axis=1)
    cs = jnp.cumsum(full.astype(jnp.float32), axis=1)
    cs = jnp.pad(cs, ((0, 0), (1, 0), (0, 0)))
    pos = start_pos + jnp.arange(T)
    outs = []
    for gi, w in enumerate(POOL_WINDOWS):
        lo, hi = gi * POOL_GROUP, (gi + 1) * POOL_GROUP
        csg = cs[..., lo:hi]
        s = csg[:, POOL_BUF + 1:POOL_BUF + 1 + T] - csg[:, POOL_BUF + 1 - w:POOL_BUF + 1 - w + T]
        cnt = jnp.minimum(w, pos + 1).astype(jnp.float32)[None, :, None]
        d = (s / cnt - u[..., lo:hi].astype(jnp.float32)).astype(u.dtype)
        outs.append(jnp.einsum('btc,cd->btd', d, w_pool[gi]))
    y = jnp.concatenate(outs, axis=-1) * pool_scale
    return y, full[:, -POOL_BUF:]


def causal_conv(xb, buf, w_conv, b_conv):
    T = xb.shape[1]
    full = jnp.concatenate([buf.astype(xb.dtype), xb], axis=1)
    y = b_conv
    for k in range(CONV_W):
        y = y + full[:, k:k + T] * w_conv[k]
    return y, full[:, -(CONV_W - 1):]


def rg_lru(xc, h0, w_a, b_a, w_x, b_x, lam):
    B, T, _ = xc.shape
    xh = xc.reshape(B, T, N_LRU_HEADS, LRU_HEAD)
    r = jax.nn.sigmoid(jnp.einsum('bthi,hij->bthj', xh, w_a).reshape(B, T, D_LRU) + b_a)
    ig = jax.nn.sigmoid(jnp.einsum('bthi,hij->bthj', xh, w_x).reshape(B, T, D_LRU) + b_x)
    log_a = -LRU_C * r.astype(jnp.float32) * jax.nn.softplus(-lam.astype(jnp.float32))
    a = jnp.exp(log_a)
    mult = jnp.sqrt(-jnp.expm1(2.0 * log_a))
    b = mult * (ig * xc).astype(jnp.float32)

    def step(h, ab):
        a_t, b_t = ab
        h = a_t * h + b_t
        return h, h

    hT, hs = lax.scan(step, h0.astype(jnp.float32), (jnp.swapaxes(a, 0, 1), jnp.swapaxes(b, 0, 1)))
    return jnp.swapaxes(hs, 0, 1).astype(xc.dtype), hT.astype(h0.dtype)


def peer(xn, w_q, key1, key2, w_u, w_v):
    B, T, D = xn.shape
    n = B * T
    blk = min(PEER_BLOCK, n)
    nb = -(-n // blk)
    xp = jnp.pad(xn.reshape(n, D), ((0, nb * blk - n), (0, 0))).reshape(nb, blk, D)

    def block(xb):
        q = (xb @ w_q).reshape(blk, PEER_HEADS, 2, D_HALF)
        s1 = jnp.einsum('thd,hkd->thk', q[:, :, 0], key1).astype(jnp.float32)
        s2 = jnp.einsum('thd,hkd->thk', q[:, :, 1], key2).astype(jnp.float32)
        v1, i1 = lax.top_k(s1, PEER_TOPK)
        v2, i2 = lax.top_k(s2, PEER_TOPK)
        cand_s = (v1[..., :, None] + v2[..., None, :]).reshape(blk, PEER_HEADS, PEER_TOPK * PEER_TOPK)
        cand_i = (i1[..., :, None] * N_KEYS + i2[..., None, :]).reshape(blk, PEER_HEADS, PEER_TOPK * PEER_TOPK)
        s, j = lax.top_k(cand_s, PEER_TOPK)
        idx = jnp.take_along_axis(cand_i, j, axis=-1)
        g = jax.nn.softmax(s, axis=-1)
        act = jax.nn.gelu(jnp.einsum('td,thkd->thk', xb, w_u[idx]).astype(jnp.float32))
        coef = (g * act).astype(xb.dtype)
        return jnp.einsum('thk,thkd->td', coef, w_v[idx])

    y = lax.map(block, xp).reshape(nb * blk, D)[:n]
    return y.reshape(B, T, D)


def trunk(x, pool_buf, conv_buf, h_state, start_pos, norm_mix, w_in, w_pool, pool_scale,
          conv_w, conv_b, lru_wa, lru_ba, lru_wx, lru_bx, lru_lambda, w_out, norm_ffn,
          peer_wq, peer_key1, peer_key2, peer_wu, peer_wv, norm_final):
    new_pool, new_conv, new_h = [], [], []
    for l in range(DEPTH):
        z = rmsnorm(x, norm_mix[l]) @ w_in[l]
        u_pool = z[..., :D_POOL]
        u_lru = z[..., D_POOL:D_POOL + D_LRU]
        gate = z[..., D_POOL + D_LRU:]
        y_pool, pb = pool_mixer(u_pool, pool_buf[l], start_pos, w_pool[l], pool_scale[l])
        xc, cb = causal_conv(u_lru, conv_buf[l], conv_w[l], conv_b[l])
        hs, hT = rg_lru(xc, h_state[l], lru_wa[l], lru_ba[l], lru_wx[l], lru_bx[l], lru_lambda[l])
        y_lru = hs * jax.nn.gelu(gate)
        x = x + jnp.concatenate([y_pool, y_lru], axis=-1) @ w_out[l]
        x = x + peer(rmsnorm(x, norm_ffn[l]), peer_wq[l], peer_key1[l], peer_key2[l], peer_wu[l], peer_wv[l])
        new_pool.append(pb)
        new_conv.append(cb)
        new_h.append(hT)
    return (rmsnorm(x, norm_final), jnp.stack(new_pool), jnp.stack(new_conv), jnp.stack(new_h))


def setup_inputs(seed: int = 0) -> dict:
    key = jax.random.key(seed)
    ks = jax.random.split(key, 32)
    f32 = jnp.float32
    nrm = lambda k, s, sc: jax.random.normal(k, s, f32) * sc
    lam_u = jax.random.uniform(ks[14], (DEPTH, D_LRU), f32, 0.9, 0.999)
    p = lam_u ** (1.0 / LRU_C)
    lru_lambda = jnp.log(p) - jnp.log1p(-p)
    return {
        'x_prompt': nrm(ks[0], (BATCH, SEQ, D_MODEL), 1.0),
        'x_sample': nrm(ks[1], (DEC_BATCH, DEC_SEQ, D_MODEL), 1.0),
        'cache_pool': nrm(ks[2], (DEPTH, DEC_BATCH, POOL_BUF, D_POOL), 1.0),
        'state_conv': nrm(ks[3], (DEPTH, DEC_BATCH, CONV_W - 1, D_LRU), 1.0),
        'state_lru': nrm(ks[4], (DEPTH, DEC_BATCH, D_LRU), 0.5),
        'norm_mix': 1.0 + nrm(ks[5], (DEPTH, D_MODEL), 0.02),
        'w_in': nrm(ks[6], (DEPTH, D_MODEL, D_IN), D_MODEL ** -0.5),
        'w_pool': nrm(ks[7], (DEPTH, N_POOL_GROUPS, POOL_GROUP, POOL_GROUP), POOL_GROUP ** -0.5),
        'pool_scale': 1.0 + nrm(ks[8], (DEPTH, D_POOL), 0.02),
        'conv_w': nrm(ks[9], (DEPTH, CONV_W, D_LRU), CONV_W ** -0.5),
        'conv_b': nrm(ks[10], (DEPTH, D_LRU), 0.01),
        'lru_wa': nrm(ks[11], (DEPTH, N_LRU_HEADS, LRU_HEAD, LRU_HEAD), LRU_HEAD ** -0.5),
        'lru_ba': nrm(ks[12], (DEPTH, D_LRU), 0.01),
        'lru_wx': nrm(ks[13], (DEPTH, N_LRU_HEADS, LRU_HEAD, LRU_HEAD), LRU_HEAD ** -0.5),
        'lru_bx': nrm(ks[15], (DEPTH, D_LRU), 0.01),
        'lru_lambda': lru_lambda,
        'w_out': nrm(ks[16], (DEPTH, D_MIX, D_MODEL), D_MIX ** -0.5),
        'norm_ffn': 1.0 + nrm(ks[17], (DEPTH, D_MODEL), 0.02),
        'peer_wq': nrm(ks[18], (DEPTH, D_MODEL, PEER_HEADS * D_KEY), D_MODEL ** -0.5),
        'peer_key1': nrm(ks[19], (DEPTH, PEER_HEADS, N_KEYS, D_HALF), D_HALF ** -0.5),
        'peer_key2': nrm(ks[20], (DEPTH, PEER_HEADS, N_KEYS, D_HALF), D_HALF ** -0.5),
        'peer_wu': nrm(ks[21], (DEPTH, N_EXPERTS, D_MODEL), D_MODEL ** -0.5),
        'peer_wv': nrm(ks[22], (DEPTH, N_EXPERTS, D_MODEL), (PEER_HEADS * PEER_TOPK) ** -0.5),
        'norm_final': 1.0 + nrm(ks[23], (D_MODEL,), 0.02),
    }


def reference(x_prompt, x_sample, cache_pool, state_conv, state_lru, norm_mix, w_in, w_pool,
              pool_scale, conv_w, conv_b, lru_wa, lru_ba, lru_wx, lru_bx, lru_lambda, w_out,
              norm_ffn, peer_wq, peer_key1, peer_key2, peer_wu, peer_wv, norm_final):
    dt = x_prompt.dtype
    pool0 = jnp.zeros((DEPTH, BATCH, POOL_BUF, D_POOL), dt)
    conv0 = jnp.zeros((DEPTH, BATCH, CONV_W - 1, D_LRU), dt)
    h0 = jnp.zeros((DEPTH, BATCH, D_LRU), state_lru.dtype)
    y_prompt, new_pool_prompt, new_conv_prompt, new_lru_prompt = trunk(
        x_prompt, pool0, conv0, h0, 0, norm_mix, w_in, w_pool, pool_scale, conv_w, conv_b,
        lru_wa, lru_ba, lru_wx, lru_bx, lru_lambda, w_out, norm_ffn, peer_wq, peer_key1,
        peer_key2, peer_wu, peer_wv, norm_final)
    y_sample, new_pool_sample, new_conv_sample, new_lru_sample = trunk(
        x_sample, cache_pool, state_conv, state_lru, PAST_LEN, norm_mix, w_in, w_pool, pool_scale,
        conv_w, conv_b, lru_wa, lru_ba, lru_wx, lru_bx, lru_lambda, w_out, norm_ffn, peer_wq,
        peer_key1, peer_key2, peer_wu, peer_wv, norm_final)
    return (y_prompt, y_sample, new_pool_prompt, new_conv_prompt, new_lru_prompt,
            new_pool_sample, new_conv_sample, new_lru_sample)
```

```python
import functools
import math

import jax
import jax.numpy as jnp
from jax import lax
from jax.experimental import pallas as pl
from jax.experimental.pallas import tpu as pltpu

F32 = jnp.float32
BF16 = jnp.bfloat16

D_MODEL = 1024
D_POOL = 512
D_LRU = 512
D_IN = D_POOL + 2 * D_LRU
POOL_WINDOWS = (2, 4, 8, 16)
POOL_GROUP = 128
POOL_BUF = 15
CONV_W = 4
N_LRU_HEADS = 8
LRU_HEAD = 64
LRU_C = 8.0
PEER_HEADS = 8
N_KEYS = 128
N_EXPERTS = N_KEYS * N_KEYS
D_HALF = 128
PEER_TOPK = 16
N_SLOTS = PEER_HEADS * PEER_TOPK
EPS = 1e-6

LANES = 128
SUBLANES = 8
ROWS_PER_EXPERT = D_MODEL // (2 * LANES)
HIGH_HALF_MASK = -65536
PAST_LEN = 1024
VMEM_LIMIT = 56 * 1024 * 1024

POOL_PAD = 16
CONV_PAD = 8
TOKEN_BLOCK = 256


def _gelu_tanh(x):
    return 0.5 * x * (1.0 + jnp.tanh(math.sqrt(2.0 / math.pi) * (x + 0.044715 * (x * x * x))))


def _sigmoid(x):
    return 1.0 / (1.0 + jnp.exp(-x))


def _rms_scale(x):
    return lax.rsqrt(jnp.mean(x * x, axis=-1, keepdims=True) + EPS)


def _mixer_kernel(x_ref, pool0_ref, conv0_ref, h0_ref, gmix_ref, win_ref, wpool_ref, pscale_ref,
                  convw_ref, convb_ref, wax_ref, bax_ref, lam_ref, wout_ref,
                  x1_ref, npool_ref, nconv_ref, nh_ref,
                  pext, cext, sa0, sb0, sa1, sb1, hcar,
                  *, chunk, t_valid, start_pos, scan_pad):
    c = pl.program_id(1)
    last_chunk = (t_valid - 1) // chunk
    last_row = (t_valid - 1) % chunk

    @pl.when(c == 0)
    def _():
        pext[0:POOL_PAD - POOL_BUF, :] = jnp.zeros((POOL_PAD - POOL_BUF, D_POOL), F32)
        pext[POOL_PAD - POOL_BUF:POOL_PAD, :] = pool0_ref[...]
        cext[0:CONV_PAD - (CONV_W - 1), :] = jnp.zeros((CONV_PAD - (CONV_W - 1), D_LRU), F32)
        cext[CONV_PAD - (CONV_W - 1):CONV_PAD, :] = conv0_ref[...]
        hcar[...] = h0_ref[...]
        sa0[0:scan_pad, :] = jnp.ones((scan_pad, D_LRU), F32)
        sa1[0:scan_pad, :] = jnp.ones((scan_pad, D_LRU), F32)
        sb0[0:scan_pad, :] = jnp.zeros((scan_pad, D_LRU), F32)
        sb1[0:scan_pad, :] = jnp.zeros((scan_pad, D_LRU), F32)

    x = x_ref[...]
    xn = (x * _rms_scale(x)) * gmix_ref[...]
    z = jnp.dot(xn.astype(BF16), win_ref[...], preferred_element_type=F32)

    pext[POOL_PAD:POOL_PAD + chunk, :] = z[:, :D_POOL]
    pos = start_pos + c * chunk + lax.broadcasted_iota(jnp.int32, (chunk, 1), 0)
    y_parts = []
    for gi, w in enumerate(POOL_WINDOWS):
        lo = gi * POOL_GROUP
        u_g = pext[POOL_PAD:POOL_PAD + chunk, lo:lo + POOL_GROUP]
        s = u_g
        for k in range(1, w):
            s = s + pext[POOL_PAD - k:POOL_PAD - k + chunk, lo:lo + POOL_GROUP]
        cnt = jnp.minimum(w, pos + 1).astype(F32)
        d = s / cnt - u_g
        y_parts.append(jnp.dot(d.astype(BF16), wpool_ref[gi], preferred_element_type=F32))

    cext[CONV_PAD:CONV_PAD + chunk, :] = z[:, D_POOL:D_POOL + D_LRU]
    xc = convb_ref[...]
    for k in range(CONV_W):
        off = CONV_PAD - (CONV_W - 1) + k
        xc = xc + cext[off:off + chunk, :] * convw_ref[k:k + 1, :]
    gates = jnp.dot(xc.astype(BF16), wax_ref[...], preferred_element_type=F32) + bax_ref[...]
    r = _sigmoid(gates[:, :D_LRU])
    ig = _sigmoid(gates[:, D_LRU:])
    nlam = -lam_ref[...]
    softplus = jnp.maximum(nlam, 0.0) + jnp.log(1.0 + jnp.exp(-jnp.abs(nlam)))
    log_a = (-LRU_C) * r * softplus
    a = jnp.exp(log_a)
    mult = jnp.sqrt(1.0 - jnp.exp(2.0 * log_a))
    bb = mult * (ig * xc)

    bufs = ((sa0, sb0), (sa1, sb1))
    sa0[scan_pad:scan_pad + chunk, :] = a
    sb0[scan_pad:scan_pad + chunk, :] = bb
    cur = 0
    shift = 1
    while shift < chunk:
        src_a, src_b = bufs[cur]
        dst_a, dst_b = bufs[1 - cur]
        for j in range(D_LRU // LANES):
            cs = slice(j * LANES, (j + 1) * LANES)
            a_cur = src_a[scan_pad:scan_pad + chunk, cs]
            b_cur = src_b[scan_pad:scan_pad + chunk, cs]
            a_sh = src_a[scan_pad - shift:scan_pad - shift + chunk, cs]
            b_sh = src_b[scan_pad - shift:scan_pad - shift + chunk, cs]
            dst_a[scan_pad:scan_pad + chunk, cs] = a_cur * a_sh
            dst_b[scan_pad:scan_pad + chunk, cs] = a_cur * b_sh + b_cur
        cur = 1 - cur
        shift *= 2
    fin_a, fin_b = bufs[cur]
    hs = fin_a[scan_pad:scan_pad + chunk, :] * hcar[...] + fin_b[scan_pad:scan_pad + chunk, :]
    y_lru = hs * _gelu_tanh(z[:, D_POOL + D_LRU:])

    y_pool = jnp.concatenate(y_parts, axis=-1) * pscale_ref[...]
    ycat = jnp.concatenate([y_pool, y_lru], axis=-1).astype(BF16)
    x1_ref[...] = x + jnp.dot(ycat, wout_ref[...], preferred_element_type=F32)

    @pl.when(c == last_chunk)
    def _():
        npool_ref[...] = pext[POOL_PAD + last_row - (POOL_BUF - 1):POOL_PAD + last_row + 1, :]
        nconv_ref[...] = cext[CONV_PAD + last_row - (CONV_W - 2):CONV_PAD + last_row + 1, :]
        nh_ref[...] = hs[last_row:last_row + 1, :]

    hcar[...] = hs[chunk - 1:chunk, :]
    pext[0:POOL_PAD, :] = pext[chunk:chunk + POOL_PAD, :]
    cext[0:CONV_PAD, :] = cext[chunk:chunk + CONV_PAD, :]


def _mixer(x, pool0, conv0, h0, params, *, chunk, t_valid, start_pos):
    B, T, _ = x.shape
    assert T % chunk == 0 and chunk % SUBLANES == 0 and chunk >= POOL_PAD
    assert (t_valid - 1) % chunk >= POOL_BUF - 1, "state rows must sit inside the last valid chunk"
    n_chunks = T // chunk
    scan_pad = max(chunk // 2, SUBLANES)
    gmix, win, wpool, pscale, convw, convb, wax, bax, lam, wout = params
    kern = functools.partial(_mixer_kernel, chunk=chunk, t_valid=t_valid, start_pos=start_pos,
                             scan_pad=scan_pad)

    def full(shape):
        return pl.BlockSpec(shape, lambda b, c: (0,) * len(shape))

    return pl.pallas_call(
        kern,
        grid=(B, n_chunks),
        in_specs=[
            pl.BlockSpec((None, chunk, D_MODEL), lambda b, c: (b, c, 0)),
            pl.BlockSpec((None, POOL_BUF, D_POOL), lambda b, c: (b, 0, 0)),
            pl.BlockSpec((None, CONV_W - 1, D_LRU), lambda b, c: (b, 0, 0)),
            pl.BlockSpec((None, 1, D_LRU), lambda b, c: (b, 0, 0)),
            full((1, D_MODEL)), full((D_MODEL, D_IN)), full((len(POOL_WINDOWS), POOL_GROUP, POOL_GROUP)),
            full((1, D_POOL)), full((CONV_W, D_LRU)), full((1, D_LRU)), full((D_LRU, 2 * D_LRU)),
            full((1, 2 * D_LRU)), full((1, D_LRU)), full((D_MODEL, D_MODEL)),
        ],
        out_specs=[
            pl.BlockSpec((None, chunk, D_MODEL), lambda b, c: (b, c, 0)),
            pl.BlockSpec((None, POOL_BUF, D_POOL), lambda b, c: (b, 0, 0)),
            pl.BlockSpec((None, CONV_W - 1, D_LRU), lambda b, c: (b, 0, 0)),
            pl.BlockSpec((None, 1, D_LRU), lambda b, c: (b, 0, 0)),
        ],
        out_shape=[
            jax.ShapeDtypeStruct((B, T, D_MODEL), F32),
            jax.ShapeDtypeStruct((B, POOL_BUF, D_POOL), F32),
            jax.ShapeDtypeStruct((B, CONV_W - 1, D_LRU), F32),
            jax.ShapeDtypeStruct((B, 1, D_LRU), F32),
        ],
        scratch_shapes=[
            pltpu.VMEM((chunk + POOL_PAD, D_POOL), F32),
            pltpu.VMEM((chunk + CONV_PAD, D_LRU), F32),
            pltpu.VMEM((chunk + scan_pad, D_LRU), F32),
            pltpu.VMEM((chunk + scan_pad, D_LRU), F32),
            pltpu.VMEM((chunk + scan_pad, D_LRU), F32),
            pltpu.VMEM((chunk + scan_pad, D_LRU), F32),
            pltpu.VMEM((1, D_LRU), F32),
        ],
        compiler_params=pltpu.CompilerParams(
            dimension_semantics=("arbitrary", "arbitrary"), vmem_limit_bytes=VMEM_LIMIT),
        name="mixer",
    )(x, pool0, conv0, h0, gmix, win, wpool, pscale, convw, convb, wax, bax, lam, wout)


def _top_rows(s, k, rows):
    n_rows = float(s.shape[0])
    vals, ids = [], []
    for _ in range(k):
        m = jnp.max(s, axis=0, keepdims=True)
        ii = jnp.min(jnp.where(s == m, rows, n_rows), axis=0, keepdims=True)
        vals.append(m)
        ids.append(ii)
        s = jnp.where(rows == ii, -jnp.inf, s)
    return vals, ids


_PIECES = ([(0, 16)] + [(a, 8) for a in range(1, 8)])


def _router_kernel(x1_ref, gffn_ref, wq_ref, key1_ref, key2_ref, xn_ref, idx_ref, g_ref, gt_scr):
    tb = x1_ref.shape[0]
    x1 = x1_ref[...]
    xn = (x1 * _rms_scale(x1)) * gffn_ref[...]
    xn_ref[...] = xn
    q = jnp.dot(xn.astype(BF16), wq_ref[...], preferred_element_type=F32).astype(BF16)

    rows128 = lax.broadcasted_iota(jnp.int32, (N_KEYS, tb), 0).astype(F32)
    n_cand = sum(n for _, n in _PIECES) + 8
    rows_c = lax.broadcasted_iota(jnp.int32, (n_cand, tb), 0).astype(F32)
    sub8 = lax.broadcasted_iota(jnp.int32, (8, tb), 0)
    nt_dims = (((1,), (1,)), ((), ()))

    for h in range(PEER_HEADS):
        base = h * 2 * D_HALF
        s1 = lax.dot_general(key1_ref[h], q[:, base:base + D_HALF], nt_dims, preferred_element_type=F32)
        s2 = lax.dot_general(key2_ref[h], q[:, base + D_HALF:base + 2 * D_HALF], nt_dims,
                             preferred_element_type=F32)
        v1, i1 = _top_rows(s1, PEER_TOPK, rows128)
        v2, i2 = _top_rows(s2, PEER_TOPK, rows128)
        v2s = jnp.concatenate(v2, axis=0)
        i2s = jnp.concatenate(i2, axis=0)
        cand, eid = [], []
        for a, n in _PIECES:
            limit = PEER_TOPK // (a + 1)
            cv = v1[a] + v2s[0:n]
            if limit < n:
                cv = jnp.where(sub8 < limit, cv, -jnp.inf)
            cand.append(cv)
            eid.append(i1[a] * float(N_KEYS) + i2s[0:n])
        cand.append(jnp.concatenate(v1[8:16], axis=0) + v2[0])
        eid.append(jnp.concatenate(i1[8:16], axis=0) * float(N_KEYS) + i2[0])
        cand = jnp.concatenate(cand, axis=0)
        eid = jnp.concatenate(eid, axis=0)

        vals, ids = [], []
        s = cand
        for _ in range(PEER_TOPK):
            m = jnp.max(s, axis=0, keepdims=True)
            ii = jnp.min(jnp.where(s == m, rows_c, float(n_cand)), axis=0, keepdims=True)
            hit = rows_c == ii
            vals.append(m)
            ids.append(jnp.sum(jnp.where(hit, eid, 0.0), axis=0, keepdims=True))
            s = jnp.where(hit, -jnp.inf, s)

        ex = [jnp.exp(v - vals[0]) for v in vals]
        den = ex[0]
        for e in ex[1:]:
            den = den + e
        for i in range(PEER_TOPK):
            row = h * PEER_TOPK + i
            idx_ref[row:row + 1, :] = ids[i].astype(jnp.int32)
            gt_scr[row:row + 1, :] = ex[i] / den

    g_ref[...] = gt_scr[...].T


def _router(x1, gffn, wq, key1, key2):
    n = x1.shape[0]
    tb = TOKEN_BLOCK
    assert n % tb == 0

    def full(shape):
        return pl.BlockSpec(shape, lambda i: (0,) * len(shape))

    return pl.pallas_call(
        _router_kernel,
        grid=(n // tb,),
        in_specs=[
            pl.BlockSpec((tb, D_MODEL), lambda i: (i, 0)),
            full((1, D_MODEL)), full((D_MODEL, PEER_HEADS * 2 * D_HALF)),
            full((PEER_HEADS, N_KEYS, D_HALF)), full((PEER_HEADS, N_KEYS, D_HALF)),
        ],
        out_specs=[
            pl.BlockSpec((tb, D_MODEL), lambda i: (i, 0)),
            pl.BlockSpec((N_SLOTS, tb), lambda i: (0, i)),
            pl.BlockSpec((tb, N_SLOTS), lambda i: (i, 0)),
        ],
        out_shape=[
            jax.ShapeDtypeStruct((n, D_MODEL), F32),
            jax.ShapeDtypeStruct((N_SLOTS, n), jnp.int32),
            jax.ShapeDtypeStruct((n, N_SLOTS), F32),
        ],
        scratch_shapes=[pltpu.VMEM((N_SLOTS, tb), F32)],
        compiler_params=pltpu.CompilerParams(
            dimension_semantics=("arbitrary",), vmem_limit_bytes=VMEM_LIMIT),
        name="router",
    )(x1, gffn, wq, key1, key2)


def _pack_table(w):
    wb = lax.bitcast_convert_type(w.astype(BF16), jnp.uint16).astype(jnp.uint32)
    wb = wb.reshape(N_EXPERTS, 2, ROWS_PER_EXPERT, LANES)
    packed = lax.bitcast_convert_type(wb[:, 0] | (wb[:, 1] << 16), jnp.int32)
    return packed.reshape(N_EXPERTS * ROWS_PER_EXPERT, LANES)


def _load_expert(tbl_ref, e):
    start = pl.multiple_of(e * ROWS_PER_EXPERT, ROWS_PER_EXPERT)
    slab = tbl_ref[pl.ds(start, ROWS_PER_EXPERT), :]
    lo = lax.bitcast_convert_type(slab << 16, F32)
    hi = lax.bitcast_convert_type(slab & HIGH_HALF_MASK, F32)
    return lo, hi


def _peer_u_kernel(idx_ref, xn_ref, g_ref, tbl_ref, coef_ref, pbuf, act_scr):
    tb = xn_ref.shape[0]

    def token(t, carry):
        x8 = xn_ref[t]
        xlo = x8[0:ROWS_PER_EXPERT]
        xhi = x8[ROWS_PER_EXPERT:2 * ROWS_PER_EXPERT]
        for k in range(N_SLOTS):
            lo, hi = _load_expert(tbl_ref, idx_ref[k, t])
            pbuf[k * ROWS_PER_EXPERT:(k + 1) * ROWS_PER_EXPERT, :] = lo * xlo + hi * xhi
        part = pbuf[pl.ds(0, N_SLOTS, stride=ROWS_PER_EXPERT), :]
        for r in range(1, ROWS_PER_EXPERT):
            part = part + pbuf[pl.ds(r, N_SLOTS, stride=ROWS_PER_EXPERT), :]
        act_scr[pl.ds(t, 1), :] = jnp.sum(part.T, axis=0, keepdims=True)
        return carry

    lax.fori_loop(0, tb, token, 0)
    coef_ref[...] = g_ref[...] * _gelu_tanh(act_scr[...])


def _peer_v_kernel(idx_ref, coef_ref, x1_ref, gfin_ref, tbl_ref, y_ref):
    tb = x1_ref.shape[0]
    n_acc = 4

    def token(t, carry):
        acc_lo = [None] * n_acc
        acc_hi = [None] * n_acc
        for k in range(N_SLOTS):
            lo, hi = _load_expert(tbl_ref, idx_ref[k, t])
            cf = coef_ref[t, k]
            j = k % n_acc
            acc_lo[j] = cf * lo if acc_lo[j] is None else acc_lo[j] + cf * lo
            acc_hi[j] = cf * hi if acc_hi[j] is None else acc_hi[j] + cf * hi
        out_lo = (acc_lo[0] + acc_lo[1]) + (acc_lo[2] + acc_lo[3])
        out_hi = (acc_hi[0] + acc_hi[1]) + (acc_hi[2] + acc_hi[3])
        v = x1_ref[t] + jnp.concatenate([out_lo, out_hi], axis=0)
        ss = jnp.sum(jnp.sum(v * v, axis=1, keepdims=True), axis=0, keepdims=True)
        y_ref[t] = (v * lax.rsqrt(ss * (1.0 / D_MODEL) + EPS)) * gfin_ref[...]
        return carry

    lax.fori_loop(0, tb, token, 0)


def _resident(shape):
    return pl.BlockSpec(shape, lambda i: (0,) * len(shape), pipeline_mode=pl.Buffered(1))


def _peer_u(idx_t, xn3, g, tbl):
    n = xn3.shape[0]
    tb = TOKEN_BLOCK
    return pl.pallas_call(
        _peer_u_kernel,
        grid=(n // tb,),
        in_specs=[
            pl.BlockSpec((N_SLOTS, tb), lambda i: (0, i), memory_space=pltpu.SMEM),
            pl.BlockSpec((tb, SUBLANES, LANES), lambda i: (i, 0, 0)),
            pl.BlockSpec((tb, N_SLOTS), lambda i: (i, 0)),
            _resident(tbl.shape),
        ],
        out_specs=pl.BlockSpec((tb, N_SLOTS), lambda i: (i, 0)),
        out_shape=jax.ShapeDtypeStruct((n, N_SLOTS), F32),
        scratch_shapes=[pltpu.VMEM((N_SLOTS * ROWS_PER_EXPERT, LANES), F32),
                        pltpu.VMEM((tb, N_SLOTS), F32)],
        compiler_params=pltpu.CompilerParams(
            dimension_semantics=("arbitrary",), vmem_limit_bytes=VMEM_LIMIT),
        name="peer_u",
    )(idx_t, xn3, g, tbl)


def _peer_v(idx_t, coef, x13, gfin8, tbl):
    n = x13.shape[0]
    tb = TOKEN_BLOCK
    return pl.pallas_call(
        _peer_v_kernel,
        grid=(n // tb,),
        in_specs=[
            pl.BlockSpec((N_SLOTS, tb), lambda i: (0, i), memory_space=pltpu.SMEM),
            pl.BlockSpec((tb, N_SLOTS), lambda i: (i, 0), memory_space=pltpu.SMEM),
            pl.BlockSpec((tb, SUBLANES, LANES), lambda i: (i, 0, 0)),
            pl.BlockSpec((SUBLANES, LANES), lambda i: (0, 0)),
            _resident(tbl.shape),
        ],
        out_specs=pl.BlockSpec((tb, SUBLANES, LANES), lambda i: (i, 0, 0)),
        out_shape=jax.ShapeDtypeStruct((n, SUBLANES, LANES), F32),
        compiler_params=pltpu.CompilerParams(
            dimension_semantics=("arbitrary",), vmem_limit_bytes=VMEM_LIMIT),
        name="peer_v",
    )(idx_t, coef, x13, gfin8, tbl)


def _block_diag(w):
    h, d, _ = w.shape
    eye = jnp.eye(h, dtype=w.dtype)
    return (eye[:, None, :, None] * w[:, :, None, :]).reshape(h * d, h * d)


def kernel(x_prompt, x_sample, cache_pool, state_conv, state_lru, norm_mix, w_in, w_pool, pool_scale,
           conv_w, conv_b, lru_wa, lru_ba, lru_wx, lru_bx, lru_lambda, w_out, norm_ffn, peer_wq,
           peer_key1, peer_key2, peer_wu, peer_wv, norm_final):
    assert norm_mix.shape[0] == 1, "single-layer trunk"
    B, T, _ = x_prompt.shape
    Bs, Ts, _ = x_sample.shape

    wax = jnp.concatenate([_block_diag(lru_wa[0]), _block_diag(lru_wx[0])], axis=1).astype(BF16)
    bax = jnp.concatenate([lru_ba[0], lru_bx[0]])[None, :]
    mix_params = (norm_mix, w_in[0].astype(BF16), w_pool[0].astype(BF16), pool_scale, conv_w[0],
                  conv_b, wax, bax, lru_lambda, w_out[0].astype(BF16))

    zeros = lambda *s: jnp.zeros(s, F32)
    x1_p, npool_p, nconv_p, nh_p = _mixer(
        x_prompt, zeros(B, POOL_BUF, D_POOL), zeros(B, CONV_W - 1, D_LRU), zeros(B, 1, D_LRU),
        mix_params, chunk=256, t_valid=T, start_pos=0)

    ts_pad = 128
    xs_pad = jnp.pad(x_sample, ((0, 0), (0, ts_pad - Ts), (0, 0)))
    x1_s, npool_s, nconv_s, nh_s = _mixer(
        xs_pad, cache_pool[0], state_conv[0], state_lru[0][:, None, :],
        mix_params, chunk=ts_pad, t_valid=Ts, start_pos=PAST_LEN)

    n_p, n_s = B * T, Bs * Ts
    x1 = jnp.concatenate([x1_p.reshape(n_p, D_MODEL), x1_s[:, :Ts].reshape(n_s, D_MODEL)], axis=0)
    n = n_p + n_s
    assert n % TOKEN_BLOCK == 0

    xn, idx_t, g = _router(x1, norm_ffn, peer_wq[0].astype(BF16), peer_key1[0].astype(BF16),
                           peer_key2[0].astype(BF16))
    coef = _peer_u(idx_t, xn.reshape(n, SUBLANES, LANES), g, _pack_table(peer_wu[0]))
    y = _peer_v(idx_t, coef, x1.reshape(n, SUBLANES, LANES), norm_final.reshape(SUBLANES, LANES),
                _pack_table(peer_wv[0])).reshape(n, D_MODEL)

    return (y[:n_p].reshape(B, T, D_MODEL), y[n_p:].reshape(Bs, Ts, D_MODEL),
            npool_p[None], nconv_p[None], nh_p.reshape(1, B, D_LRU),
            npool_s[None], nconv_s[None], nh_s.reshape(1, Bs, D_LRU))
```

```python
import functools
import math

import jax
import jax.numpy as jnp
from jax import lax
from jax.experimental import pallas as pl
from jax.experimental.pallas import tpu as pltpu

F32 = jnp.float32
BF16 = jnp.bfloat16

D_MODEL = 1024
D_POOL = 512
D_LRU = 512
D_IN = D_POOL + 2 * D_LRU
POOL_WINDOWS = (2, 4, 8, 16)
POOL_GROUP = 128
POOL_BUF = 15
CONV_W = 4
N_LRU_HEADS = 8
LRU_HEAD = 64
LRU_C = 8.0
PEER_HEADS = 8
N_KEYS = 128
N_EXPERTS = N_KEYS * N_KEYS
D_HALF = 128
PEER_TOPK = 16
N_SLOTS = PEER_HEADS * PEER_TOPK
EPS = 1e-6

LANES = 128
SUBLANES = 8
ROWS_PER_EXPERT = D_MODEL // (2 * LANES)
HIGH_HALF_MASK = -65536
PAST_LEN = 1024
VMEM_LIMIT = 56 * 1024 * 1024

POOL_PAD = 16
CONV_PAD = 8
TOKEN_BLOCK = 256


def _gelu_tanh(x):
    return 0.5 * x * (1.0 + jnp.tanh(math.sqrt(2.0 / math.pi) * (x + 0.044715 * (x * x * x))))


def _sigmoid(x):
    return 1.0 / (1.0 + jnp.exp(-x))


def _rms_scale(x):
    return lax.rsqrt(jnp.mean(x * x, axis=-1, keepdims=True) + EPS)


def _mixer_kernel(x_ref, pool0_ref, conv0_ref, h0_ref, gmix_ref, win_ref, wpool_ref, pscale_ref,
                  convw_ref, convb_ref, wax_ref, bax_ref, lam_ref, wout_ref,
                  x1_ref, npool_ref, nconv_ref, nh_ref,
                  pext, cext, sa0, sb0, sa1, sb1, hcar,
                  *, chunk, t_valid, start_pos, scan_pad):
    c = pl.program_id(1)
    last_chunk = (t_valid - 1) // chunk
    last_row = (t_valid - 1) % chunk

    @pl.when(c == 0)
    def _():
        pext[0:POOL_PAD - POOL_BUF, :] = jnp.zeros((POOL_PAD - POOL_BUF, D_POOL), F32)
        pext[POOL_PAD - POOL_BUF:POOL_PAD, :] = pool0_ref[...]
        cext[0:CONV_PAD - (CONV_W - 1), :] = jnp.zeros((CONV_PAD - (CONV_W - 1), D_LRU), F32)
        cext[CONV_PAD - (CONV_W - 1):CONV_PAD, :] = conv0_ref[...]
        hcar[...] = h0_ref[...]
        sa0[0:scan_pad, :] = jnp.ones((scan_pad, D_LRU), F32)
        sa1[0:scan_pad, :] = jnp.ones((scan_pad, D_LRU), F32)
        sb0[0:scan_pad, :] = jnp.zeros((scan_pad, D_LRU), F32)
        sb1[0:scan_pad, :] = jnp.zeros((scan_pad, D_LRU), F32)

    x = x_ref[...]
    xn = (x * _rms_scale(x)) * gmix_ref[...]
    z = jnp.dot(xn.astype(BF16), win_ref[...], preferred_element_type=F32)

    pext[POOL_PAD:POOL_PAD + chunk, :] = z[:, :D_POOL]
    pos = start_pos + c * chunk + lax.broadcasted_iota(jnp.int32, (chunk, 1), 0)
    y_parts = []
    for gi, w in enumerate(POOL_WINDOWS):
        lo = gi * POOL_GROUP
        u_g = pext[POOL_PAD:POOL_PAD + chunk, lo:lo + POOL_GROUP]
        s = u_g
        for k in range(1, w):
            s = s + pext[POOL_PAD - k:POOL_PAD - k + chunk, lo:lo + POOL_GROUP]
        cnt = jnp.minimum(w, pos + 1).astype(F32)
        d = s / cnt - u_g
        y_parts.append(jnp.dot(d.astype(BF16), wpool_ref[gi], preferred_element_type=F32))

    cext[CONV_PAD:CONV_PAD + chunk, :] = z[:, D_POOL:D_POOL + D_LRU]
    xc = convb_ref[...]
    for k in range(CONV_W):
        off = CONV_PAD - (CONV_W - 1) + k
        xc = xc + cext[off:off + chunk, :] * convw_ref[k:k + 1, :]
    gates = jnp.dot(xc.astype(BF16), wax_ref[...], preferred_element_type=F32) + bax_ref[...]
    r = _sigmoid(gates[:, :D_LRU])
    ig = _sigmoid(gates[:, D_LRU:])
    nlam = -lam_ref[...]
    softplus = jnp.maximum(nlam, 0.0) + jnp.log(1.0 + jnp.exp(-jnp.abs(nlam)))
    log_a = (-LRU_C) * r * softplus
    a = jnp.exp(log_a)
    mult = jnp.sqrt(1.0 - jnp.exp(2.0 * log_a))
    bb = mult * (ig * xc)

    bufs = ((sa0, sb0), (sa1, sb1))
    sa0[scan_pad:scan_pad + chunk, :] = a
    sb0[scan_pad:scan_pad + chunk, :] = bb
    cur = 0
    shift = 1
    while shift < chunk:
        src_a, src_b = bufs[cur]
        dst_a, dst_b = bufs[1 - cur]
        for j in range(D_LRU // LANES):
            cs = slice(j * LANES, (j + 1) * LANES)
            a_cur = src_a[scan_pad:scan_pad + chunk, cs]
            b_cur = src_b[scan_pad:scan_pad + chunk, cs]
            a_sh = src_a[scan_pad - shift:scan_pad - shift + chunk, cs]
            b_sh = src_b[scan_pad - shift:scan_pad - shift + chunk, cs]
            dst_a[scan_pad:scan_pad + chunk, cs] = a_cur * a_sh
            dst_b[scan_pad:scan_pad + chunk, cs] = a_cur * b_sh + b_cur
        cur = 1 - cur
        shift *= 2
    fin_a, fin_b = bufs[cur]
    hs = fin_a[scan_pad:scan_pad + chunk, :] * hcar[...] + fin_b[scan_pad:scan_pad + chunk, :]
    y_lru = hs * _gelu_tanh(z[:, D_POOL + D_LRU:])

    y_pool = jnp.concatenate(y_parts, axis=-1) * pscale_ref[...]
    ycat = jnp.concatenate([y_pool, y_lru], axis=-1).astype(BF16)
    x1_ref[...] = x + jnp.dot(ycat, wout_ref[...], preferred_element_type=F32)

    @pl.when(c == last_chunk)
    def _():
        npool_ref[...] = pext[POOL_PAD + last_row - (POOL_BUF - 1):POOL_PAD + last_row + 1, :]
        nconv_ref[...] = cext[CONV_PAD + last_row - (CONV_W - 2):CONV_PAD + last_row + 1, :]
        nh_ref[...] = hs[last_row:last_row + 1, :]

    hcar[...] = hs[chunk - 1:chunk, :]
    pext[0:POOL_PAD, :] = pext[chunk:chunk + POOL_PAD, :]
    cext[0:CONV_PAD, :] = cext[chunk:chunk + CONV_PAD, :]


def _mixer(x, pool0, conv0, h0, params, *, chunk, t_valid, start_pos):
    B, T, _ = x.shape
    assert T % chunk == 0 and chunk % SUBLANES == 0 and chunk >= POOL_PAD
    assert (t_valid - 1) % chunk >= POOL_BUF - 1, "state rows must sit inside the last valid chunk"
    n_chunks = T // chunk
    scan_pad = max(chunk // 2, SUBLANES)
    gmix, win, wpool, pscale, convw, convb, wax, bax, lam, wout = params
    kern = functools.partial(_mixer_kernel, chunk=chunk, t_valid=t_valid, start_pos=start_pos,
                             scan_pad=scan_pad)

    def full(shape):
        return pl.BlockSpec(shape, lambda b, c: (0,) * len(shape))

    return pl.pallas_call(
        kern,
        grid=(B, n_chunks),
        in_specs=[
            pl.BlockSpec((None, chunk, D_MODEL), lambda b, c: (b, c, 0)),
            pl.BlockSpec((None, POOL_BUF, D_POOL), lambda b, c: (b, 0, 0)),
            pl.BlockSpec((None, CONV_W - 1, D_LRU), lambda b, c: (b, 0, 0)),
            pl.BlockSpec((None, 1, D_LRU), lambda b, c: (b, 0, 0)),
            full((1, D_MODEL)), full((D_MODEL, D_IN)), full((len(POOL_WINDOWS), POOL_GROUP, POOL_GROUP)),
            full((1, D_POOL)), full((CONV_W, D_LRU)), full((1, D_LRU)), full((D_LRU, 2 * D_LRU)),
            full((1, 2 * D_LRU)), full((1, D_LRU)), full((D_MODEL, D_MODEL)),
        ],
        out_specs=[
            pl.BlockSpec((None, chunk, D_MODEL), lambda b, c: (b, c, 0)),
            pl.BlockSpec((None, POOL_BUF, D_POOL), lambda b, c: (b, 0, 0)),
            pl.BlockSpec((None, CONV_W - 1, D_LRU), lambda b, c: (b, 0, 0)),
            pl.BlockSpec((None, 1, D_LRU), lambda b, c: (b, 0, 0)),
        ],
        out_shape=[
            jax.ShapeDtypeStruct((B, T, D_MODEL), F32),
            jax.ShapeDtypeStruct((B, POOL_BUF, D_POOL), F32),
            jax.ShapeDtypeStruct((B, CONV_W - 1, D_LRU), F32),
            jax.ShapeDtypeStruct((B, 1, D_LRU), F32),
        ],
        scratch_shapes=[
            pltpu.VMEM((chunk + POOL_PAD, D_POOL), F32),
            pltpu.VMEM((chunk + CONV_PAD, D_LRU), F32),
            pltpu.VMEM((chunk + scan_pad, D_LRU), F32),
            pltpu.VMEM((chunk + scan_pad, D_LRU), F32),
            pltpu.VMEM((chunk + scan_pad, D_LRU), F32),
            pltpu.VMEM((chunk + scan_pad, D_LRU), F32),
            pltpu.VMEM((1, D_LRU), F32),
        ],
        compiler_params=pltpu.CompilerParams(
            dimension_semantics=("arbitrary", "arbitrary"), vmem_limit_bytes=VMEM_LIMIT),
        name="mixer",
    )(x, pool0, conv0, h0, gmix, win, wpool, pscale, convw, convb, wax, bax, lam, wout)


def _top_rows(s, k, rows):
    n_rows = float(s.shape[0])
    vals, ids = [], []
    for _ in range(k):
        m = jnp.max(s, axis=0, keepdims=True)
        ii = jnp.min(jnp.where(s == m, rows, n_rows), axis=0, keepdims=True)
        vals.append(m)
        ids.append(ii)
        s = jnp.where(rows == ii, -jnp.inf, s)
    return vals, ids


_PIECES = ([(0, 16)] + [(a, 8) for a in range(1, 8)])


def _router_kernel(x1_ref, gffn_ref, wq_ref, key1_ref, key2_ref, xn_ref, idx_ref, g_ref, gt_scr):
    tb = x1_ref.shape[0]
    x1 = x1_ref[...]
    xn = (x1 * _rms_scale(x1)) * gffn_ref[...]
    xn_ref[...] = xn
    q = jnp.dot(xn.astype(BF16), wq_ref[...], preferred_element_type=F32).astype(BF16)

    rows128 = lax.broadcasted_iota(jnp.int32, (N_KEYS, tb), 0).astype(F32)
    n_cand = sum(n for _, n in _PIECES) + 8
    rows_c = lax.broadcasted_iota(jnp.int32, (n_cand, tb), 0).astype(F32)
    sub8 = lax.broadcasted_iota(jnp.int32, (8, tb), 0)
    nt_dims = (((1,), (1,)), ((), ()))

    for h in range(PEER_HEADS):
        base = h * 2 * D_HALF
        s1 = lax.dot_general(key1_ref[h], q[:, base:base + D_HALF], nt_dims, preferred_element_type=F32)
        s2 = lax.dot_general(key2_ref[h], q[:, base + D_HALF:base + 2 * D_HALF], nt_dims,
                             preferred_element_type=F32)
        v1, i1 = _top_rows(s1, PEER_TOPK, rows128)
        v2, i2 = _top_rows(s2, PEER_TOPK, rows128)
        v2s = jnp.concatenate(v2, axis=0)
        i2s = jnp.concatenate(i2, axis=0)
        cand, eid = [], []
        for a, n in _PIECES:
            limit = PEER_TOPK // (a + 1)
            cv = v1[a] + v2s[0:n]
            if limit < n:
                cv = jnp.where(sub8 < limit, cv, -jnp.inf)
            cand.append(cv)
            eid.append(i1[a] * float(N_KEYS) + i2s[0:n])
        cand.append(jnp.concatenate(v1[8:16], axis=0) + v2[0])
        eid.append(jnp.concatenate(i1[8:16], axis=0) * float(N_KEYS) + i2[0])
        cand = jnp.concatenate(cand, axis=0)
        eid = jnp.concatenate(eid, axis=0)

        vals, ids = [], []
        s = cand
        for _ in range(PEER_TOPK):
            m = jnp.max(s, axis=0, keepdims=True)
            ii = jnp.min(jnp.where(s == m, rows_c, float(n_cand)), axis=0, keepdims=True)
            hit = rows_c == ii
            vals.append(m)
            ids.append(jnp.sum(jnp.where(hit, eid, 0.0), axis=0, keepdims=True))
            s = jnp.where(hit, -jnp.inf, s)

        ex = [jnp.exp(v - vals[0]) for v in vals]
        den = ex[0]
        for e in ex[1:]:
            den = den + e
        for i in range(PEER_TOPK):
            row = h * PEER_TOPK + i
            idx_ref[row:row + 1, :] = ids[i].astype(jnp.int32) * ROWS_PER_EXPERT
            gt_scr[row:row + 1, :] = ex[i] / den

    g_ref[...] = gt_scr[...].T


def _router(x1, gffn, wq, key1, key2):
    n = x1.shape[0]
    tb = TOKEN_BLOCK
    assert n % tb == 0

    def full(shape):
        return pl.BlockSpec(shape, lambda i: (0,) * len(shape))

    return pl.pallas_call(
        _router_kernel,
        grid=(n // tb,),
        in_specs=[
            pl.BlockSpec((tb, D_MODEL), lambda i: (i, 0)),
            full((1, D_MODEL)), full((D_MODEL, PEER_HEADS * 2 * D_HALF)),
            full((PEER_HEADS, N_KEYS, D_HALF)), full((PEER_HEADS, N_KEYS, D_HALF)),
        ],
        out_specs=[
            pl.BlockSpec((tb, D_MODEL), lambda i: (i, 0)),
            pl.BlockSpec((N_SLOTS, tb), lambda i: (0, i)),
            pl.BlockSpec((tb, N_SLOTS), lambda i: (i, 0)),
        ],
        out_shape=[
            jax.ShapeDtypeStruct((n, D_MODEL), F32),
            jax.ShapeDtypeStruct((N_SLOTS, n), jnp.int32),
            jax.ShapeDtypeStruct((n, N_SLOTS), F32),
        ],
        scratch_shapes=[pltpu.VMEM((N_SLOTS, tb), F32)],
        compiler_params=pltpu.CompilerParams(
            dimension_semantics=("arbitrary",), vmem_limit_bytes=VMEM_LIMIT),
        name="router",
    )(x1, gffn, wq, key1, key2)


def _pack_table(w):
    wb = lax.bitcast_convert_type(w.astype(BF16), jnp.uint16).astype(jnp.uint32)
    wb = wb.reshape(N_EXPERTS, 2, ROWS_PER_EXPERT, LANES)
    packed = lax.bitcast_convert_type(wb[:, 0] | (wb[:, 1] << 16), jnp.int32)
    return packed.reshape(N_EXPERTS * ROWS_PER_EXPERT, LANES)


def _load_expert(tbl_ref, row0):
    slab = tbl_ref[pl.ds(pl.multiple_of(row0, ROWS_PER_EXPERT), ROWS_PER_EXPERT), :]
    lo = lax.bitcast_convert_type(slab << 16, F32)
    hi = lax.bitcast_convert_type(slab & HIGH_HALF_MASK, F32)
    return lo, hi


def _peer_u_kernel(idx_ref, xn_ref, g_ref, tbl_ref, coef_ref, pbuf_even, pbuf_odd, act_scr):
    tb = xn_ref.shape[0]
    assert tb % 2 == 0

    def products(t, pbuf):
        x8 = xn_ref[t]
        xlo = x8[0:ROWS_PER_EXPERT]
        xhi = x8[ROWS_PER_EXPERT:2 * ROWS_PER_EXPERT]
        for k in range(N_SLOTS):
            lo, hi = _load_expert(tbl_ref, idx_ref[k, t])
            pbuf[k * ROWS_PER_EXPERT:(k + 1) * ROWS_PER_EXPERT, :] = lo * xlo + hi * xhi

    def reduce(pbuf):
        part = pbuf[pl.ds(0, N_SLOTS, stride=ROWS_PER_EXPERT), :]
        for r in range(1, ROWS_PER_EXPERT):
            part = part + pbuf[pl.ds(r, N_SLOTS, stride=ROWS_PER_EXPERT), :]
        return jnp.sum(part.T, axis=0, keepdims=True)

    pbuf_odd[...] = jnp.zeros(pbuf_odd.shape, F32)

    def token_pair(i, carry):
        t = 2 * i
        products(t, pbuf_even)
        act_scr[pl.ds(jnp.maximum(t - 1, 0), 1), :] = reduce(pbuf_odd)
        products(t + 1, pbuf_odd)
        act_scr[pl.ds(t, 1), :] = reduce(pbuf_even)
        return carry

    lax.fori_loop(0, tb // 2, token_pair, 0)
    act_scr[tb - 1:tb, :] = reduce(pbuf_odd)
    coef_ref[...] = g_ref[...] * _gelu_tanh(act_scr[...])


def _peer_v_kernel(idx_ref, coef_ref, x1_ref, gfin_ref, tbl_ref, y_ref, cb_even, cb_odd):
    tb = x1_ref.shape[0]
    assert tb % 2 == 0
    n_acc = 4

    def spread_coef(t, cb):
        row = coef_ref[pl.ds(t, 1), :]
        cb[...] = jnp.broadcast_to(row, (N_SLOTS, N_SLOTS)).T

    def combine(t, cb):
        acc_lo = [None] * n_acc
        acc_hi = [None] * n_acc
        for k in range(N_SLOTS):
            lo, hi = _load_expert(tbl_ref, idx_ref[k, t])
            cf = cb[k:k + 1, :]
            j = k % n_acc
            acc_lo[j] = cf * lo if acc_lo[j] is None else acc_lo[j] + cf * lo
            acc_hi[j] = cf * hi if acc_hi[j] is None else acc_hi[j] + cf * hi
        out_lo = (acc_lo[0] + acc_lo[1]) + (acc_lo[2] + acc_lo[3])
        out_hi = (acc_hi[0] + acc_hi[1]) + (acc_hi[2] + acc_hi[3])
        v = x1_ref[t] + jnp.concatenate([out_lo, out_hi], axis=0)
        ss = jnp.sum(jnp.sum(v * v, axis=1, keepdims=True), axis=0, keepdims=True)
        y_ref[t] = (v * lax.rsqrt(ss * (1.0 / D_MODEL) + EPS)) * gfin_ref[...]

    spread_coef(0, cb_even)

    def token_pair(i, carry):
        t = 2 * i
        spread_coef(t + 1, cb_odd)
        combine(t, cb_even)
        spread_coef(jnp.minimum(t + 2, tb - 1), cb_even)
        combine(t + 1, cb_odd)
        return carry

    lax.fori_loop(0, tb // 2, token_pair, 0)


def _resident(shape):
    return pl.BlockSpec(shape, lambda i: (0,) * len(shape), pipeline_mode=pl.Buffered(1))


def _peer_u(idx_t, xn3, g, tbl):
    n = xn3.shape[0]
    tb = TOKEN_BLOCK
    return pl.pallas_call(
        _peer_u_kernel,
        grid=(n // tb,),
        in_specs=[
            pl.BlockSpec((N_SLOTS, tb), lambda i: (0, i), memory_space=pltpu.SMEM),
            pl.BlockSpec((tb, SUBLANES, LANES), lambda i: (i, 0, 0)),
            pl.BlockSpec((tb, N_SLOTS), lambda i: (i, 0)),
            _resident(tbl.shape),
        ],
        out_specs=pl.BlockSpec((tb, N_SLOTS), lambda i: (i, 0)),
        out_shape=jax.ShapeDtypeStruct((n, N_SLOTS), F32),
        scratch_shapes=[pltpu.VMEM((N_SLOTS * ROWS_PER_EXPERT, LANES), F32),
                        pltpu.VMEM((N_SLOTS * ROWS_PER_EXPERT, LANES), F32),
                        pltpu.VMEM((tb, N_SLOTS), F32)],
        compiler_params=pltpu.CompilerParams(
            dimension_semantics=("arbitrary",), vmem_limit_bytes=VMEM_LIMIT),
        name="peer_u",
    )(idx_t, xn3, g, tbl)


def _peer_v(idx_t, coef, x13, gfin8, tbl):
    n = x13.shape[0]
    tb = TOKEN_BLOCK
    return pl.pallas_call(
        _peer_v_kernel,
        grid=(n // tb,),
        in_specs=[
            pl.BlockSpec((N_SLOTS, tb), lambda i: (0, i), memory_space=pltpu.SMEM),
            pl.BlockSpec((tb, N_SLOTS), lambda i: (i, 0)),
            pl.BlockSpec((tb, SUBLANES, LANES), lambda i: (i, 0, 0)),
            pl.BlockSpec((SUBLANES, LANES), lambda i: (0, 0)),
            _resident(tbl.shape),
        ],
        out_specs=pl.BlockSpec((tb, SUBLANES, LANES), lambda i: (i, 0, 0)),
        out_shape=jax.ShapeDtypeStruct((n, SUBLANES, LANES), F32),
        scratch_shapes=[pltpu.VMEM((N_SLOTS, N_SLOTS), F32), pltpu.VMEM((N_SLOTS, N_SLOTS), F32)],
        compiler_params=pltpu.CompilerParams(
            dimension_semantics=("arbitrary",), vmem_limit_bytes=VMEM_LIMIT),
        name="peer_v",
    )(idx_t, coef, x13, gfin8, tbl)


def _block_diag(w):
    h, d, _ = w.shape
    eye = jnp.eye(h, dtype=w.dtype)
    return (eye[:, None, :, None] * w[:, :, None, :]).reshape(h * d, h * d)


def kernel(x_prompt, x_sample, cache_pool, state_conv, state_lru, norm_mix, w_in, w_pool, pool_scale,
           conv_w, conv_b, lru_wa, lru_ba, lru_wx, lru_bx, lru_lambda, w_out, norm_ffn, peer_wq,
           peer_key1, peer_key2, peer_wu, peer_wv, norm_final):
    assert norm_mix.shape[0] == 1, "single-layer trunk"
    B, T, _ = x_prompt.shape
    Bs, Ts, _ = x_sample.shape

    wax = jnp.concatenate([_block_diag(lru_wa[0]), _block_diag(lru_wx[0])], axis=1).astype(BF16)
    bax = jnp.concatenate([lru_ba[0], lru_bx[0]])[None, :]
    mix_params = (norm_mix, w_in[0].astype(BF16), w_pool[0].astype(BF16), pool_scale, conv_w[0],
                  conv_b, wax, bax, lru_lambda, w_out[0].astype(BF16))

    zeros = lambda *s: jnp.zeros(s, F32)
    x1_p, npool_p, nconv_p, nh_p = _mixer(
        x_prompt, zeros(B, POOL_BUF, D_POOL), zeros(B, CONV_W - 1, D_LRU), zeros(B, 1, D_LRU),
        mix_params, chunk=256, t_valid=T, start_pos=0)

    ts_pad = 128
    xs_pad = jnp.pad(x_sample, ((0, 0), (0, ts_pad - Ts), (0, 0)))
    x1_s, npool_s, nconv_s, nh_s = _mixer(
        xs_pad, cache_pool[0], state_conv[0], state_lru[0][:, None, :],
        mix_params, chunk=ts_pad, t_valid=Ts, start_pos=PAST_LEN)

    n_p, n_s = B * T, Bs * Ts
    x1 = jnp.concatenate([x1_p.reshape(n_p, D_MODEL), x1_s[:, :Ts].reshape(n_s, D_MODEL)], axis=0)
    n = n_p + n_s
    assert n % TOKEN_BLOCK == 0

    xn, idx_t, g = _router(x1, norm_ffn, peer_wq[0].astype(BF16), peer_key1[0].astype(BF16),
                           peer_key2[0].astype(BF16))
    coef = _peer_u(idx_t, xn.reshape(n, SUBLANES, LANES), g, _pack_table(peer_wu[0]))
    y = _peer_v(idx_t, coef, x1.reshape(n, SUBLANES, LANES), norm_final.reshape(SUBLANES, LANES),
                _pack_table(peer_wv[0])).reshape(n, D_MODEL)

    return (y[:n_p].reshape(B, T, D_MODEL), y[n_p:].reshape(Bs, Ts, D_MODEL),
            npool_p[None], nconv_p[None], nh_p.reshape(1, B, D_LRU),
            npool_s[None], nconv_s[None], nh_s.reshape(1, Bs, D_LRU))
```

```python
import functools
import math

import jax
import jax.numpy as jnp
from jax import lax
from jax.experimental import pallas as pl
from jax.experimental.pallas import tpu as pltpu

F32 = jnp.float32
BF16 = jnp.bfloat16

D_MODEL = 1024
D_POOL = 512
D_LRU = 512
D_IN = D_POOL + 2 * D_LRU
POOL_WINDOWS = (2, 4, 8, 16)
POOL_GROUP = 128
POOL_BUF = 15
CONV_W = 4
N_LRU_HEADS = 8
LRU_HEAD = 64
LRU_C = 8.0
PEER_HEADS = 8
N_KEYS = 128
N_EXPERTS = N_KEYS * N_KEYS
D_HALF = 128
PEER_TOPK = 16
N_SLOTS = PEER_HEADS * PEER_TOPK
EPS = 1e-6

LANES = 128
SUBLANES = 8
ROWS_PER_EXPERT = D_MODEL // (2 * LANES)
HIGH_HALF_MASK = -65536
PAST_LEN = 1024
VMEM_LIMIT = 56 * 1024 * 1024

POOL_PAD = 16
CONV_PAD = 8
TOKEN_BLOCK = 256


def _gelu_tanh(x):
    return 0.5 * x * (1.0 + jnp.tanh(math.sqrt(2.0 / math.pi) * (x + 0.044715 * (x * x * x))))


def _sigmoid(x):
    return 1.0 / (1.0 + jnp.exp(-x))


def _rms_scale(x):
    return lax.rsqrt(jnp.mean(x * x, axis=-1, keepdims=True) + EPS)


def _mixer_kernel(x_ref, pool0_ref, conv0_ref, h0_ref, gmix_ref, win_ref, wpool_ref, pscale_ref,
                  convw_ref, convb_ref, wax_ref, bax_ref, lam_ref, wout_ref,
                  x1_ref, npool_ref, nconv_ref, nh_ref,
                  pext, cext, sa0, sb0, sa1, sb1, hcar,
                  *, chunk, t_valid, start_pos, scan_pad):
    c = pl.program_id(1)
    last_chunk = (t_valid - 1) // chunk
    last_row = (t_valid - 1) % chunk

    @pl.when(c == 0)
    def _():
        pext[0:POOL_PAD - POOL_BUF, :] = jnp.zeros((POOL_PAD - POOL_BUF, D_POOL), F32)
        pext[POOL_PAD - POOL_BUF:POOL_PAD, :] = pool0_ref[...]
        cext[0:CONV_PAD - (CONV_W - 1), :] = jnp.zeros((CONV_PAD - (CONV_W - 1), D_LRU), F32)
        cext[CONV_PAD - (CONV_W - 1):CONV_PAD, :] = conv0_ref[...]
        hcar[...] = h0_ref[...]
        sa0[0:scan_pad, :] = jnp.ones((scan_pad, D_LRU), F32)
        sa1[0:scan_pad, :] = jnp.ones((scan_pad, D_LRU), F32)
        sb0[0:scan_pad, :] = jnp.zeros((scan_pad, D_LRU), F32)
        sb1[0:scan_pad, :] = jnp.zeros((scan_pad, D_LRU), F32)

    x = x_ref[...]
    xn = (x * _rms_scale(x)) * gmix_ref[...]
    z = jnp.dot(xn.astype(BF16), win_ref[...], preferred_element_type=F32)

    pext[POOL_PAD:POOL_PAD + chunk, :] = z[:, :D_POOL]
    pos = start_pos + c * chunk + lax.broadcasted_iota(jnp.int32, (chunk, 1), 0)
    y_parts = []
    for gi, w in enumerate(POOL_WINDOWS):
        lo = gi * POOL_GROUP
        u_g = pext[POOL_PAD:POOL_PAD + chunk, lo:lo + POOL_GROUP]
        s = u_g
        for k in range(1, w):
            s = s + pext[POOL_PAD - k:POOL_PAD - k + chunk, lo:lo + POOL_GROUP]
        cnt = jnp.minimum(w, pos + 1).astype(F32)
        d = s / cnt - u_g
        y_parts.append(jnp.dot(d.astype(BF16), wpool_ref[gi], preferred_element_type=F32))

    cext[CONV_PAD:CONV_PAD + chunk, :] = z[:, D_POOL:D_POOL + D_LRU]
    xc = convb_ref[...]
    for k in range(CONV_W):
        off = CONV_PAD - (CONV_W - 1) + k
        xc = xc + cext[off:off + chunk, :] * convw_ref[k:k + 1, :]
    gates = jnp.dot(xc.astype(BF16), wax_ref[...], preferred_element_type=F32) + bax_ref[...]
    r = _sigmoid(gates[:, :D_LRU])
    ig = _sigmoid(gates[:, D_LRU:])
    nlam = -lam_ref[...]
    softplus = jnp.maximum(nlam, 0.0) + jnp.log(1.0 + jnp.exp(-jnp.abs(nlam)))
    log_a = (-LRU_C) * r * softplus
    a = jnp.exp(log_a)
    mult = jnp.sqrt(1.0 - jnp.exp(2.0 * log_a))
    bb = mult * (ig * xc)

    bufs = ((sa0, sb0), (sa1, sb1))
    sa0[scan_pad:scan_pad + chunk, :] = a
    sb0[scan_pad:scan_pad + chunk, :] = bb
    cur = 0
    shift = 1
    while shift < chunk:
        src_a, src_b = bufs[cur]
        dst_a, dst_b = bufs[1 - cur]
        for j in range(D_LRU // LANES):
            cs = slice(j * LANES, (j + 1) * LANES)
            a_cur = src_a[scan_pad:scan_pad + chunk, cs]
            b_cur = src_b[scan_pad:scan_pad + chunk, cs]
            a_sh = src_a[scan_pad - shift:scan_pad - shift + chunk, cs]
            b_sh = src_b[scan_pad - shift:scan_pad - shift + chunk, cs]
            dst_a[scan_pad:scan_pad + chunk, cs] = a_cur * a_sh
            dst_b[scan_pad:scan_pad + chunk, cs] = a_cur * b_sh + b_cur
        cur = 1 - cur
        shift *= 2
    fin_a, fin_b = bufs[cur]
    hs = fin_a[scan_pad:scan_pad + chunk, :] * hcar[...] + fin_b[scan_pad:scan_pad + chunk, :]
    y_lru = hs * _gelu_tanh(z[:, D_POOL + D_LRU:])

    y_pool = jnp.concatenate(y_parts, axis=-1) * pscale_ref[...]
    ycat = jnp.concatenate([y_pool, y_lru], axis=-1).astype(BF16)
    x1_ref[...] = x + jnp.dot(ycat, wout_ref[...], preferred_element_type=F32)

    @pl.when(c == last_chunk)
    def _():
        npool_ref[...] = pext[POOL_PAD + last_row - (POOL_BUF - 1):POOL_PAD + last_row + 1, :]
        nconv_ref[...] = cext[CONV_PAD + last_row - (CONV_W - 2):CONV_PAD + last_row + 1, :]
        nh_ref[...] = hs[last_row:last_row + 1, :]

    hcar[...] = hs[chunk - 1:chunk, :]
    pext[0:POOL_PAD, :] = pext[chunk:chunk + POOL_PAD, :]
    cext[0:CONV_PAD, :] = cext[chunk:chunk + CONV_PAD, :]


def _mixer(x, pool0, conv0, h0, params, *, chunk, t_valid, start_pos):
    B, T, _ = x.shape
    assert T % chunk == 0 and chunk % SUBLANES == 0 and chunk >= POOL_PAD
    assert (t_valid - 1) % chunk >= POOL_BUF - 1, "state rows must sit inside the last valid chunk"
    n_chunks = T // chunk
    scan_pad = max(chunk // 2, SUBLANES)
    gmix, win, wpool, pscale, convw, convb, wax, bax, lam, wout = params
    kern = functools.partial(_mixer_kernel, chunk=chunk, t_valid=t_valid, start_pos=start_pos,
                             scan_pad=scan_pad)

    def full(shape):
        return pl.BlockSpec(shape, lambda b, c: (0,) * len(shape))

    return pl.pallas_call(
        kern,
        grid=(B, n_chunks),
        in_specs=[
            pl.BlockSpec((None, chunk, D_MODEL), lambda b, c: (b, c, 0)),
            pl.BlockSpec((None, POOL_BUF, D_POOL), lambda b, c: (b, 0, 0)),
            pl.BlockSpec((None, CONV_W - 1, D_LRU), lambda b, c: (b, 0, 0)),
            pl.BlockSpec((None, 1, D_LRU), lambda b, c: (b, 0, 0)),
            full((1, D_MODEL)), full((D_MODEL, D_IN)), full((len(POOL_WINDOWS), POOL_GROUP, POOL_GROUP)),
            full((1, D_POOL)), full((CONV_W, D_LRU)), full((1, D_LRU)), full((D_LRU, 2 * D_LRU)),
            full((1, 2 * D_LRU)), full((1, D_LRU)), full((D_MODEL, D_MODEL)),
        ],
        out_specs=[
            pl.BlockSpec((None, chunk, D_MODEL), lambda b, c: (b, c, 0)),
            pl.BlockSpec((None, POOL_BUF, D_POOL), lambda b, c: (b, 0, 0)),
            pl.BlockSpec((None, CONV_W - 1, D_LRU), lambda b, c: (b, 0, 0)),
            pl.BlockSpec((None, 1, D_LRU), lambda b, c: (b, 0, 0)),
        ],
        out_shape=[
            jax.ShapeDtypeStruct((B, T, D_MODEL), F32),
            jax.ShapeDtypeStruct((B, POOL_BUF, D_POOL), F32),
            jax.ShapeDtypeStruct((B, CONV_W - 1, D_LRU), F32),
            jax.ShapeDtypeStruct((B, 1, D_LRU), F32),
        ],
        scratch_shapes=[
            pltpu.VMEM((chunk + POOL_PAD, D_POOL), F32),
            pltpu.VMEM((chunk + CONV_PAD, D_LRU), F32),
            pltpu.VMEM((chunk + scan_pad, D_LRU), F32),
            pltpu.VMEM((chunk + scan_pad, D_LRU), F32),
            pltpu.VMEM((chunk + scan_pad, D_LRU), F32),
            pltpu.VMEM((chunk + scan_pad, D_LRU), F32),
            pltpu.VMEM((1, D_LRU), F32),
        ],
        compiler_params=pltpu.CompilerParams(
            dimension_semantics=("arbitrary", "arbitrary"), vmem_limit_bytes=VMEM_LIMIT),
        name="mixer",
    )(x, pool0, conv0, h0, gmix, win, wpool, pscale, convw, convb, wax, bax, lam, wout)


def _top_rows(s, k, rows):
    n_rows = float(s.shape[0])
    vals, ids = [], []
    for _ in range(k):
        m = jnp.max(s, axis=0, keepdims=True)
        ii = jnp.min(jnp.where(s == m, rows, n_rows), axis=0, keepdims=True)
        vals.append(m)
        ids.append(ii)
        s = jnp.where(rows == ii, -jnp.inf, s)
    return vals, ids


_PIECES = ([(0, 16)] + [(a, 8) for a in range(1, 8)])


def _router_kernel(x1_ref, gffn_ref, wq_ref, key1_ref, key2_ref, xn_ref, idx_ref, g_ref, gt_scr):
    tb = x1_ref.shape[0]
    x1 = x1_ref[...]
    xn = (x1 * _rms_scale(x1)) * gffn_ref[...]
    xn_ref[...] = xn
    q = jnp.dot(xn.astype(BF16), wq_ref[...], preferred_element_type=F32).astype(BF16)

    rows128 = lax.broadcasted_iota(jnp.int32, (N_KEYS, tb), 0).astype(F32)
    n_cand = sum(n for _, n in _PIECES) + 8
    rows_c = lax.broadcasted_iota(jnp.int32, (n_cand, tb), 0).astype(F32)
    sub8 = lax.broadcasted_iota(jnp.int32, (8, tb), 0)
    nt_dims = (((1,), (1,)), ((), ()))

    for h in range(PEER_HEADS):
        base = h * 2 * D_HALF
        s1 = lax.dot_general(key1_ref[h], q[:, base:base + D_HALF], nt_dims, preferred_element_type=F32)
        s2 = lax.dot_general(key2_ref[h], q[:, base + D_HALF:base + 2 * D_HALF], nt_dims,
                             preferred_element_type=F32)
        v1, i1 = _top_rows(s1, PEER_TOPK, rows128)
        v2, i2 = _top_rows(s2, PEER_TOPK, rows128)
        v2s = jnp.concatenate(v2, axis=0)
        i2s = jnp.concatenate(i2, axis=0)
        cand, eid = [], []
        for a, n in _PIECES:
            limit = PEER_TOPK // (a + 1)
            cv = v1[a] + v2s[0:n]
            if limit < n:
                cv = jnp.where(sub8 < limit, cv, -jnp.inf)
            cand.append(cv)
            eid.append(i1[a] * float(N_KEYS) + i2s[0:n])
        cand.append(jnp.concatenate(v1[8:16], axis=0) + v2[0])
        eid.append(jnp.concatenate(i1[8:16], axis=0) * float(N_KEYS) + i2[0])
        cand = jnp.concatenate(cand, axis=0)
        eid = jnp.concatenate(eid, axis=0)

        vals, ids = [], []
        s = cand
        for _ in range(PEER_TOPK):
            m = jnp.max(s, axis=0, keepdims=True)
            ii = jnp.min(jnp.where(s == m, rows_c, float(n_cand)), axis=0, keepdims=True)
            hit = rows_c == ii
            vals.append(m)
            ids.append(jnp.sum(jnp.where(hit, eid, 0.0), axis=0, keepdims=True))
            s = jnp.where(hit, -jnp.inf, s)

        ex = [jnp.exp(v - vals[0]) for v in vals]
        den = ex[0]
        for e in ex[1:]:
            den = den + e
        for i in range(PEER_TOPK):
            row = h * PEER_TOPK + i
            idx_ref[row:row + 1, :] = ids[i].astype(jnp.int32) * ROWS_PER_EXPERT
            gt_scr[row:row + 1, :] = ex[i] / den

    g_ref[...] = gt_scr[...].T


def _router(x1, gffn, wq, key1, key2):
    n = x1.shape[0]
    tb = TOKEN_BLOCK
    assert n % tb == 0

    def full(shape):
        return pl.BlockSpec(shape, lambda i: (0,) * len(shape))

    return pl.pallas_call(
        _router_kernel,
        grid=(n // tb,),
        in_specs=[
            pl.BlockSpec((tb, D_MODEL), lambda i: (i, 0)),
            full((1, D_MODEL)), full((D_MODEL, PEER_HEADS * 2 * D_HALF)),
            full((PEER_HEADS, N_KEYS, D_HALF)), full((PEER_HEADS, N_KEYS, D_HALF)),
        ],
        out_specs=[
            pl.BlockSpec((tb, D_MODEL), lambda i: (i, 0)),
            pl.BlockSpec((N_SLOTS, tb), lambda i: (0, i)),
            pl.BlockSpec((tb, N_SLOTS), lambda i: (i, 0)),
        ],
        out_shape=[
            jax.ShapeDtypeStruct((n, D_MODEL), F32),
            jax.ShapeDtypeStruct((N_SLOTS, n), jnp.int32),
            jax.ShapeDtypeStruct((n, N_SLOTS), F32),
        ],
        scratch_shapes=[pltpu.VMEM((N_SLOTS, tb), F32)],
        compiler_params=pltpu.CompilerParams(
            dimension_semantics=("arbitrary",), vmem_limit_bytes=VMEM_LIMIT),
        name="router",
    )(x1, gffn, wq, key1, key2)


def _pack_table(w):
    wb = lax.bitcast_convert_type(w.astype(BF16), jnp.uint16).astype(jnp.uint32)
    wb = wb.reshape(N_EXPERTS, 2, ROWS_PER_EXPERT, LANES)
    packed = lax.bitcast_convert_type(wb[:, 0] | (wb[:, 1] << 16), jnp.int32)
    return packed.reshape(N_EXPERTS * ROWS_PER_EXPERT, LANES)


def _load_expert(tbl_ref, row0):
    slab = tbl_ref[pl.ds(pl.multiple_of(row0, ROWS_PER_EXPERT), ROWS_PER_EXPERT), :]
    lo = lax.bitcast_convert_type(slab << 16, F32)
    hi = lax.bitcast_convert_type(slab & HIGH_HALF_MASK, F32)
    return lo, hi


def _peer_u_kernel(idx_ref, xn_ref, g_ref, tbl_ref, coef_ref, pbuf_even, pbuf_odd, act_scr):
    tb = xn_ref.shape[0]
    assert tb % 2 == 0

    def products(t, pbuf):
        x8 = xn_ref[t]
        xlo = x8[0:ROWS_PER_EXPERT]
        xhi = x8[ROWS_PER_EXPERT:2 * ROWS_PER_EXPERT]
        for k in range(N_SLOTS):
            lo, hi = _load_expert(tbl_ref, idx_ref[k, t])
            pbuf[k * ROWS_PER_EXPERT:(k + 1) * ROWS_PER_EXPERT, :] = lo * xlo + hi * xhi

    def reduce(pbuf):
        part = pbuf[pl.ds(0, N_SLOTS, stride=ROWS_PER_EXPERT), :]
        for r in range(1, ROWS_PER_EXPERT):
            part = part + pbuf[pl.ds(r, N_SLOTS, stride=ROWS_PER_EXPERT), :]
        return jnp.sum(part.T, axis=0, keepdims=True)

    pbuf_odd[...] = jnp.zeros(pbuf_odd.shape, F32)

    def token_pair(i, carry):
        t = 2 * i
        products(t, pbuf_even)
        act_scr[pl.ds(jnp.maximum(t - 1, 0), 1), :] = reduce(pbuf_odd)
        products(t + 1, pbuf_odd)
        act_scr[pl.ds(t, 1), :] = reduce(pbuf_even)
        return carry

    lax.fori_loop(0, tb // 2, token_pair, 0)
    act_scr[tb - 1:tb, :] = reduce(pbuf_odd)
    coef_ref[...] = g_ref[...] * _gelu_tanh(act_scr[...])


def _peer_v_kernel(idx_ref, coef_ref, x1_ref, gfin_ref, tbl_ref, y_ref, cb_even, cb_odd):
    tb = x1_ref.shape[0]
    assert tb % 2 == 0
    n_acc = 4

    def spread_coef(t, cb):
        row = coef_ref[pl.ds(t, 1), :]
        cb[...] = jnp.broadcast_to(row, (N_SLOTS, N_SLOTS)).T

    def combine(t, cb):
        acc_lo = [None] * n_acc
        acc_hi = [None] * n_acc
        for k in range(N_SLOTS):
            lo, hi = _load_expert(tbl_ref, idx_ref[k, t])
            cf = cb[k:k + 1, :]
            j = k % n_acc
            acc_lo[j] = cf * lo if acc_lo[j] is None else acc_lo[j] + cf * lo
            acc_hi[j] = cf * hi if acc_hi[j] is None else acc_hi[j] + cf * hi
        out_lo, out_hi = acc_lo[0], acc_hi[0]
        for j in range(1, n_acc):
            out_lo = out_lo + acc_lo[j]
            out_hi = out_hi + acc_hi[j]
        y_ref[t] = x1_ref[t] + jnp.concatenate([out_lo, out_hi], axis=0)

    spread_coef(0, cb_even)

    def token_pair(i, carry):
        t = 2 * i
        spread_coef(t + 1, cb_odd)
        combine(t, cb_even)
        spread_coef(jnp.minimum(t + 2, tb - 1), cb_even)
        combine(t + 1, cb_odd)
        return carry

    lax.fori_loop(0, tb // 2, token_pair, 0)

    v = y_ref[...]
    ss = jnp.sum(jnp.sum(v * v, axis=2, keepdims=True), axis=1, keepdims=True)
    y_ref[...] = (v * lax.rsqrt(ss * (1.0 / D_MODEL) + EPS)) * gfin_ref[...]


def _resident(shape):
    return pl.BlockSpec(shape, lambda i: (0,) * len(shape), pipeline_mode=pl.Buffered(1))


def _peer_u(idx_t, xn3, g, tbl):
    n = xn3.shape[0]
    tb = TOKEN_BLOCK
    return pl.pallas_call(
        _peer_u_kernel,
        grid=(n // tb,),
        in_specs=[
            pl.BlockSpec((N_SLOTS, tb), lambda i: (0, i), memory_space=pltpu.SMEM),
            pl.BlockSpec((tb, SUBLANES, LANES), lambda i: (i, 0, 0)),
            pl.BlockSpec((tb, N_SLOTS), lambda i: (i, 0)),
            _resident(tbl.shape),
        ],
        out_specs=pl.BlockSpec((tb, N_SLOTS), lambda i: (i, 0)),
        out_shape=jax.ShapeDtypeStruct((n, N_SLOTS), F32),
        scratch_shapes=[pltpu.VMEM((N_SLOTS * ROWS_PER_EXPERT, LANES), F32),
                        pltpu.VMEM((N_SLOTS * ROWS_PER_EXPERT, LANES), F32),
                        pltpu.VMEM((tb, N_SLOTS), F32)],
        compiler_params=pltpu.CompilerParams(
            dimension_semantics=("arbitrary",), vmem_limit_bytes=VMEM_LIMIT),
        name="peer_u",
    )(idx_t, xn3, g, tbl)


def _peer_v(idx_t, coef, x13, gfin8, tbl):
    n = x13.shape[0]
    tb = TOKEN_BLOCK
    return pl.pallas_call(
        _peer_v_kernel,
        grid=(n // tb,),
        in_specs=[
            pl.BlockSpec((N_SLOTS, tb), lambda i: (0, i), memory_space=pltpu.SMEM),
            pl.BlockSpec((tb, N_SLOTS), lambda i: (i, 0)),
            pl.BlockSpec((tb, SUBLANES, LANES), lambda i: (i, 0, 0)),
            pl.BlockSpec((SUBLANES, LANES), lambda i: (0, 0)),
            _resident(tbl.shape),
        ],
        out_specs=pl.BlockSpec((tb, SUBLANES, LANES), lambda i: (i, 0, 0)),
        out_shape=jax.ShapeDtypeStruct((n, SUBLANES, LANES), F32),
        scratch_shapes=[pltpu.VMEM((N_SLOTS, N_SLOTS), F32), pltpu.VMEM((N_SLOTS, N_SLOTS), F32)],
        compiler_params=pltpu.CompilerParams(
            dimension_semantics=("arbitrary",), vmem_limit_bytes=VMEM_LIMIT),
        name="peer_v",
    )(idx_t, coef, x13, gfin8, tbl)


def _block_diag(w):
    h, d, _ = w.shape
    eye = jnp.eye(h, dtype=w.dtype)
    return (eye[:, None, :, None] * w[:, :, None, :]).reshape(h * d, h * d)


def kernel(x_prompt, x_sample, cache_pool, state_conv, state_lru, norm_mix, w_in, w_pool, pool_scale,
           conv_w, conv_b, lru_wa, lru_ba, lru_wx, lru_bx, lru_lambda, w_out, norm_ffn, peer_wq,
           peer_key1, peer_key2, peer_wu, peer_wv, norm_final):
    assert norm_mix.shape[0] == 1, "single-layer trunk"
    B, T, _ = x_prompt.shape
    Bs, Ts, _ = x_sample.shape

    wax = jnp.concatenate([_block_diag(lru_wa[0]), _block_diag(lru_wx[0])], axis=1).astype(BF16)
    bax = jnp.concatenate([lru_ba[0], lru_bx[0]])[None, :]
    mix_params = (norm_mix, w_in[0].astype(BF16), w_pool[0].astype(BF16), pool_scale, conv_w[0],
                  conv_b, wax, bax, lru_lambda, w_out[0].astype(BF16))

    zeros = lambda *s: jnp.zeros(s, F32)
    x1_p, npool_p, nconv_p, nh_p = _mixer(
        x_prompt, zeros(B, POOL_BUF, D_POOL), zeros(B, CONV_W - 1, D_LRU), zeros(B, 1, D_LRU),
        mix_params, chunk=256, t_valid=T, start_pos=0)

    ts_pad = 128
    xs_pad = jnp.pad(x_sample, ((0, 0), (0, ts_pad - Ts), (0, 0)))
    x1_s, npool_s, nconv_s, nh_s = _mixer(
        xs_pad, cache_pool[0], state_conv[0], state_lru[0][:, None, :],
        mix_params, chunk=ts_pad, t_valid=Ts, start_pos=PAST_LEN)

    n_p, n_s = B * T, Bs * Ts
    x1 = jnp.concatenate([x1_p.reshape(n_p, D_MODEL), x1_s[:, :Ts].reshape(n_s, D_MODEL)], axis=0)
    n = n_p + n_s
    assert n % TOKEN_BLOCK == 0

    xn, idx_t, g = _router(x1, norm_ffn, peer_wq[0].astype(BF16), peer_key1[0].astype(BF16),
                           peer_key2[0].astype(BF16))
    coef = _peer_u(idx_t, xn.reshape(n, SUBLANES, LANES), g, _pack_table(peer_wu[0]))
    y = _peer_v(idx_t, coef, x1.reshape(n, SUBLANES, LANES), norm_final.reshape(SUBLANES, LANES),
                _pack_table(peer_wv[0])).reshape(n, D_MODEL)

    return (y[:n_p].reshape(B, T, D_MODEL), y[n_p:].reshape(Bs, Ts, D_MODEL),
            npool_p[None], nconv_p[None], nh_p.reshape(1, B, D_LRU),
            npool_s[None], nconv_s[None], nh_s.reshape(1, Bs, D_LRU))
```
